```python
import math
import jax, jax.numpy as jnp
from jax import lax
import numpy as np

D_MODEL = 1024
BATCH = 8
SEQ = 4096
DEPTH = 2

CHUNK = 64
N_BRANCHES = 4
BRANCH_WIDTH = D_MODEL // 2
S5_GROUP = 16
S5_GROUPS = BRANCH_WIDTH // S5_GROUP
S5_STATE = 64
S5_DT_MIN = 0.001
S5_DT_MAX = 0.1
SCONV_WIDTH = 3
CONF_WIDTH = 31
HGRN_HEADS = 4
HGRN_KEY = BRANCH_WIDTH // HGRN_HEADS
HGRN_VAL = BRANCH_WIDTH // HGRN_HEADS
D_FF = -(-8 * D_MODEL // (3 * 256)) * 256
IN_COLS = 10 * BRANCH_WIDTH + N_BRANCHES * D_MODEL
EPS = 1e-6

kernel_name = 'hybrid_s5_shortconv_conformer_hgrn2_gated_block'


def rms_norm(x, g):
    xf = x.astype(jnp.float32)
    y = xf * lax.rsqrt(jnp.mean(xf * xf, axis=-1, keepdims=True) + EPS) * g.astype(jnp.float32)
    return y.astype(x.dtype)


def layer_norm(x, g, b):
    xf = x.astype(jnp.float32)
    mu = jnp.mean(xf, axis=-1, keepdims=True)
    xc = xf - mu
    var = jnp.mean(xc * xc, axis=-1, keepdims=True)
    return xc * lax.rsqrt(var + EPS) * g.astype(jnp.float32) + b.astype(jnp.float32)


def causal_dwconv(x, w):
    k, c = w.shape
    return lax.conv_general_dilated(
        x, w[:, None, :].astype(x.dtype), window_strides=(1,), padding=((k - 1, 0),),
        dimension_numbers=('NWC', 'WIO', 'NWC'), feature_group_count=c)


def _lin_rec(e1, e2):
    a1, b1 = e1
    a2, b2 = e2
    return a1 * a2, a2 * b1 + b2


def s5_mixer(u, lam_re, lam_im, log_dt, b_re, b_im, c_re, c_im, d, w_glu, b_glu):
    f32 = jnp.float32
    bsz, seq, _ = u.shape
    uf = u.astype(f32)
    lam = lax.complex(lam_re.astype(f32), lam_im.astype(f32))
    dt = jnp.exp(log_dt.astype(f32))[:, None]
    lam_bar = jnp.exp(lam * dt)
    b_bar = ((lam_bar - 1.0) / lam)[..., None] * lax.complex(b_re.astype(f32), b_im.astype(f32))
    c = lax.complex(c_re.astype(f32), c_im.astype(f32))
    ug = uf.reshape(bsz, seq, S5_GROUPS, S5_GROUP).astype(jnp.complex64)
    bu = jnp.einsum('blgc,gpc->blgp', ug, b_bar)
    a = jnp.broadcast_to(lam_bar, (1, seq) + lam_bar.shape)
    _, states = lax.associative_scan(_lin_rec, (a, bu), axis=1)
    y = jnp.real(jnp.einsum('blgp,gcp->blgc', states, c)).reshape(bsz, seq, BRANCH_WIDTH)
    y = y + d.astype(f32) * uf
    y = jax.nn.gelu(y)
    y = y * jax.nn.sigmoid(y @ w_glu.astype(f32) + b_glu.astype(f32))
    return y.astype(u.dtype)


def short_conv_mixer(gb, gc, v, w):
    return gb * causal_dwconv(gc * v, w)


def conformer_conv(p_val, p_gate, dw_w, dw_b, ln_g, ln_b):
    z = p_val * jax.nn.sigmoid(p_gate)
    z = causal_dwconv(z, dw_w) + dw_b
    z = layer_norm(z, ln_g, ln_b)
    return jax.nn.silu(z).astype(p_val.dtype)


def hgrn_lower_bounds(logits):
    p = jax.nn.softmax(logits.astype(jnp.float32), axis=0)
    cs = jnp.cumsum(p, axis=0)
    return cs - cs[0:1]


def hgrn2_mixer(q, f_logit, i, g, lb, norm_g):
    f32 = jnp.float32
    bsz, seq, _ = q.shape
    n_chunks = seq // CHUNK
    qf = jax.nn.silu(q.astype(f32))
    xf = f_logit.astype(f32)
    log_f = jnp.logaddexp(jnp.log(lb), jnp.log1p(-lb) + jax.nn.log_sigmoid(xf))
    kf = (1.0 - lb) * jax.nn.sigmoid(-xf)

    def to_chunks(t, dh):
        return t.reshape(bsz, n_chunks, CHUNK, HGRN_HEADS, dh).transpose(1, 0, 3, 2, 4)

    qc = to_chunks(qf, HGRN_KEY)
    kc = to_chunks(kf, HGRN_KEY)
    gcum = jnp.cumsum(to_chunks(log_f, HGRN_KEY), axis=3)
    vc = to_chunks(i.astype(f32), HGRN_VAL)
    causal = jnp.tril(jnp.ones((CHUNK, CHUNK), dtype=bool))[:, :, None]

    def step(state, inp):
        qb, kb, gb, vb = inp
        diff = gb[:, :, :, None, :] - gb[:, :, None, :, :]
        decay = jnp.exp(jnp.where(causal, diff, -jnp.inf))
        scores = jnp.einsum('bhtk,bhsk,bhtsk->bhts', qb, kb, decay)
        out = (jnp.einsum('bhts,bhsv->bhtv', scores, vb)
               + jnp.einsum('bhtk,bhkv->bhtv', qb * jnp.exp(gb), state))
        g_last = gb[:, :, -1:, :]
        state = (jnp.exp(g_last[:, :, 0, :])[..., None] * state
                 + jnp.einsum('bhsk,bhsv->bhkv', kb * jnp.exp(g_last - gb), vb))
        return state, out

    s0 = jnp.zeros((bsz, HGRN_HEADS, HGRN_KEY, HGRN_VAL), f32)
    _, o = lax.scan(step, s0, (qc, kc, gcum, vc))
    o = o.transpose(1, 0, 3, 2, 4).reshape(bsz, seq, HGRN_HEADS, HGRN_VAL)
    o = o * lax.rsqrt(jnp.mean(o * o, axis=-1, keepdims=True) + EPS)
    o = o * norm_g.astype(f32).reshape(HGRN_HEADS, HGRN_VAL)
    o = o.reshape(bsz, seq, BRANCH_WIDTH) * jax.nn.silu(g.astype(f32))
    return o.astype(q.dtype)


def setup_inputs(seed: int = 0) -> dict:
    key = jax.random.key(seed)
    ks = jax.random.split(key, 32)
    f32 = jnp.float32
    W, D, NL = BRANCH_WIDTH, D_MODEL, DEPTH

    def nrm(k, shape, scale):
        return jax.random.normal(k, shape, f32) * scale

    def gain(k, shape):
        return 1.0 + 0.02 * jax.random.normal(k, shape, f32)

    return {
        'x': jax.random.normal(ks[0], (BATCH, SEQ, D), f32),
        'mix_norm_g': gain(ks[1], (NL, D)),
        'w_in': nrm(ks[2], (NL, D, IN_COLS), D ** -0.5),
        's5_lambda_re': -0.5 + nrm(ks[3], (NL, S5_GROUPS, S5_STATE), 0.01),
        's5_lambda_im': math.pi * jnp.arange(S5_STATE, dtype=f32) + nrm(ks[4], (NL, S5_GROUPS, S5_STATE), 0.01),
        's5_log_dt': jax.random.uniform(ks[5], (NL, S5_GROUPS), f32, math.log(S5_DT_MIN), math.log(S5_DT_MAX)),
        's5_b_re': nrm(ks[6], (NL, S5_GROUPS, S5_STATE, S5_GROUP), (2 * S5_GROUP) ** -0.5),
        's5_b_im': nrm(ks[7], (NL, S5_GROUPS, S5_STATE, S5_GROUP), (2 * S5_GROUP) ** -0.5),
        's5_c_re': nrm(ks[8], (NL, S5_GROUPS, S5_GROUP, S5_STATE), S5_STATE ** -0.5),
        's5_c_im': nrm(ks[9], (NL, S5_GROUPS, S5_GROUP, S5_STATE), S5_STATE ** -0.5),
        's5_d': nrm(ks[10], (NL, W), 1.0),
        's5_w_glu': nrm(ks[11], (NL, W, W), W ** -0.5),
        's5_b_glu': nrm(ks[12], (NL, W), 0.01),
        'sconv_w': nrm(ks[13], (NL, SCONV_WIDTH, W), SCONV_WIDTH ** -0.5),
        'conf_dw_w': nrm(ks[14], (NL, CONF_WIDTH, W), CONF_WIDTH ** -0.5),
        'conf_dw_b': nrm(ks[15], (NL, W), 0.01),
        'conf_ln_g': gain(ks[16], (NL, W)),
        'conf_ln_b': nrm(ks[17], (NL, W), 0.01),
        'hgrn_lb_logits': nrm(ks[18], (NL, HGRN_HEADS * HGRN_KEY), 0.1),
        'hgrn_norm_g': gain(ks[19], (NL, W)),
        'w_branch': nrm(ks[20], (NL, N_BRANCHES, W, D), W ** -0.5),
        'w_out': nrm(ks[21], (NL, D, D), D ** -0.5),
        'ffn_norm_g': gain(ks[22], (NL, D)),
        'ffn_w_gate': nrm(ks[23], (NL, D, D_FF), D ** -0.5),
        'ffn_w_up': nrm(ks[24], (NL, D, D_FF), D ** -0.5),
        'ffn_w_down': nrm(ks[25], (NL, D_FF, D), D_FF ** -0.5),
        'final_norm_g': gain(ks[26], (D,)),
    }


def reference(x, mix_norm_g, w_in, s5_lambda_re, s5_lambda_im, s5_log_dt, s5_b_re, s5_b_im,
              s5_c_re, s5_c_im, s5_d, s5_w_glu, s5_b_glu, sconv_w, conf_dw_w, conf_dw_b,
              conf_ln_g, conf_ln_b, hgrn_lb_logits, hgrn_norm_g, w_branch, w_out,
              ffn_norm_g, ffn_w_gate, ffn_w_up, ffn_w_down, final_norm_g):
    bsz, seq, _ = x.shape
    lower_bounds = hgrn_lower_bounds(hgrn_lb_logits)
    split_points = [BRANCH_WIDTH * j for j in range(1, 11)]
    for l in range(DEPTH):
        h = rms_norm(x, mix_norm_g[l])
        proj = h @ w_in[l]
        (u_s5, sc_b, sc_c, sc_v, cf_val, cf_gate,
         hg_q, hg_f, hg_i, hg_g, gate_logits) = jnp.split(proj, split_points, axis=-1)
        br_a = s5_mixer(u_s5, s5_lambda_re[l], s5_lambda_im[l], s5_log_dt[l], s5_b_re[l], s5_b_im[l],
                        s5_c_re[l], s5_c_im[l], s5_d[l], s5_w_glu[l], s5_b_glu[l])
        br_b = short_conv_mixer(sc_b, sc_c, sc_v, sconv_w[l])
        br_c = conformer_conv(cf_val, cf_gate, conf_dw_w[l], conf_dw_b[l], conf_ln_g[l], conf_ln_b[l])
        br_d = hgrn2_mixer(hg_q, hg_f, hg_i, hg_g, lower_bounds[l], hgrn_norm_g[l])
        gates = jax.nn.sigmoid(gate_logits.reshape(bsz, seq, N_BRANCHES, D_MODEL))
        merged = (gates[:, :, 0, :] * (br_a @ w_branch[l, 0])
                  + gates[:, :, 1, :] * (br_b @ w_branch[l, 1])
                  + gates[:, :, 2, :] * (br_c @ w_branch[l, 2])
                  + gates[:, :, 3, :] * (br_d @ w_branch[l, 3]))
        x = x + merged @ w_out[l]
        h = rms_norm(x, ffn_norm_g[l])
        x = x + (jax.nn.silu(h @ ffn_w_gate[l]) * (h @ ffn_w_up[l])) @ ffn_w_down[l]
    return rms_norm(x, final_norm_g)
```

```python
import functools
import math

import jax
import jax.numpy as jnp
from jax import lax
from jax.experimental import pallas as pl
from jax.experimental.pallas import tpu as pltpu

F32 = jnp.float32
BF16 = jnp.bfloat16

D_MODEL = 1024
BATCH = 8
SEQ = 4096
DEPTH = 2
CHUNK = 64
N_BRANCHES = 4
WIDTH = D_MODEL // 2
S5_GROUP = 16
S5_GROUPS = WIDTH // S5_GROUP
S5_STATE = 64
S5_SLABS = 4
S5_SLAB_CH = WIDTH // S5_SLABS
S5_SLAB_ST = S5_SLAB_CH // S5_GROUP * S5_STATE
SCONV_WIDTH = 3
CONF_WIDTH = 31
HGRN_HEADS = 4
HGRN_DIM = WIDTH // HGRN_HEADS
D_FF = 2816
IN_COLS = 10 * WIDTH + N_BRANCHES * D_MODEL
EPS = 1e-6

SUBLANES = 8
VMEM_LIMIT = 56 * 1024 * 1024

COL_S5, COL_SC_B, COL_SC_C, COL_SC_V, COL_CF_VAL, COL_CF_GATE = 0, 1, 2, 3, 4, 5
COL_HG_Q, COL_HG_F, COL_HG_I, COL_HG_G = 6, 7, 8, 9
GATE_COL0 = 10 * WIDTH


def _dot(a, b):
    return jnp.dot(a, b, preferred_element_type=F32)


def _rms_norm(x, g):
    return x * lax.rsqrt(jnp.mean(x * x, axis=-1, keepdims=True) + EPS) * g


def _sigmoid(x):
    return 1.0 / (1.0 + jnp.exp(-x))


def _silu(x):
    return x * _sigmoid(x)


def _const_spec(shape):
    n = len(shape)
    return pl.BlockSpec(shape, lambda *_: (0,) * n, pipeline_mode=pl.Buffered(1))


PROJ_TL = 1024
PROJ_TN = 2304
F_TILE = (COL_HG_F * WIDTH) // PROJ_TN
F_OFF = COL_HG_F * WIDTH - F_TILE * PROJ_TN


def _proj_kernel(x_ref, g_ref, w_ref, p_ref, f_ref, h_ref):
    n = pl.program_id(2)

    @pl.when(n == 0)
    def _():
        h_ref[...] = _rms_norm(x_ref[...], g_ref[...]).astype(BF16)

    acc = _dot(h_ref[...], w_ref[...])
    p_ref[...] = acc.astype(BF16)

    @pl.when(n == F_TILE)
    def _():
        f_ref[...] = acc[:, F_OFF:F_OFF + WIDTH]


def _proj(x, g, w):
    nn = IN_COLS // PROJ_TN
    return pl.pallas_call(
        _proj_kernel,
        grid=(BATCH, SEQ // PROJ_TL, nn),
        in_specs=[
            pl.BlockSpec((None, PROJ_TL, D_MODEL), lambda b, l, n: (b, l, 0)),
            pl.BlockSpec((1, D_MODEL), lambda b, l, n: (0, 0)),
            pl.BlockSpec((D_MODEL, PROJ_TN), lambda b, l, n: (0, n)),
        ],
        out_specs=[
            pl.BlockSpec((PROJ_TL, PROJ_TN), lambda b, l, n: (l, b * nn + n)),
            pl.BlockSpec((PROJ_TL, WIDTH), lambda b, l, n: (l, b)),
        ],
        out_shape=[
            jax.ShapeDtypeStruct((SEQ, BATCH * IN_COLS), BF16),
            jax.ShapeDtypeStruct((SEQ, BATCH * WIDTH), F32),
        ],
        scratch_shapes=[pltpu.VMEM((PROJ_TL, D_MODEL), BF16)],
        compiler_params=pltpu.CompilerParams(
            dimension_semantics=("arbitrary", "arbitrary", "arbitrary"),
            vmem_limit_bytes=VMEM_LIMIT),
        name="proj",
    )(x, g, w)


S5_TL = 64
S5_ROWS = S5_TL * BATCH


def _gelu_tanh(y):
    c = math.sqrt(2.0 / math.pi)
    return 0.5 * y * (1.0 + jnp.tanh(c * (y + 0.044715 * (y * y * y))))


def _s5_kernel(u_ref, bre_ref, bim_ref, cre_ref, cim_ref, lre_ref, lim_ref, d_ref,
               wglu_ref, bglu_ref, out_ref, xr_ref, xi_ref, st_ref, y_ref):
    @pl.when(pl.program_id(0) == 0)
    def _():
        st_ref[...] = jnp.zeros_like(st_ref)

    for i in range(S5_SLABS):
        lanes = slice(S5_SLAB_CH * i, S5_SLAB_CH * (i + 1))
        ui = u_ref[:, lanes]
        xr_ref[...] = _dot(ui, bre_ref[i])
        xi_ref[...] = _dot(ui, bim_ref[i])
        lr = jnp.broadcast_to(lre_ref[i], (SUBLANES, S5_SLAB_ST))
        li = jnp.broadcast_to(lim_ref[i], (SUBLANES, S5_SLAB_ST))

        def step(t, carry, lr=lr, li=li):
            sr, si = carry
            r0 = pl.multiple_of(t * SUBLANES, SUBLANES)
            nr = lr * sr - li * si + xr_ref[pl.ds(r0, SUBLANES), :]
            ni = lr * si + li * sr + xi_ref[pl.ds(r0, SUBLANES), :]
            xr_ref[pl.ds(r0, SUBLANES), :] = nr
            xi_ref[pl.ds(r0, SUBLANES), :] = ni
            return nr, ni

        sr, si = lax.fori_loop(0, S5_TL, step, (st_ref[i, 0], st_ref[i, 1]), unroll=4)
        st_ref[i, 0] = sr
        st_ref[i, 1] = si
        y_ref[:, lanes] = (_dot(xr_ref[...].astype(BF16), cre_ref[i])
                           + _dot(xi_ref[...].astype(BF16), cim_ref[i]))

    y = y_ref[...] + d_ref[...] * u_ref[...].astype(F32)
    y = _gelu_tanh(y)
    z = _dot(y.astype(BF16), wglu_ref[...]) + bglu_ref[...]
    out_ref[...] = (y * _sigmoid(z)).astype(BF16)


def _s5(p_tb, s5w):
    bre, bim, cre, cim, lre, lim, d, wglu, bglu = s5w
    return pl.pallas_call(
        _s5_kernel,
        grid=(SEQ // S5_TL,),
        in_specs=[
            pl.BlockSpec((S5_ROWS, WIDTH), lambda l: (l, COL_S5)),
            _const_spec(bre.shape), _const_spec(bim.shape),
            _const_spec(cre.shape), _const_spec(cim.shape),
            _const_spec(lre.shape), _const_spec(lim.shape),
            _const_spec(d.shape), _const_spec(wglu.shape), _const_spec(bglu.shape),
        ],
        out_specs=pl.BlockSpec((S5_ROWS, WIDTH), lambda l: (l, 0)),
        out_shape=jax.ShapeDtypeStruct((SEQ * BATCH, WIDTH), BF16),
        scratch_shapes=[
            pltpu.VMEM((S5_ROWS, S5_SLAB_ST), F32),
            pltpu.VMEM((S5_ROWS, S5_SLAB_ST), F32),
            pltpu.VMEM((S5_SLABS, 2, SUBLANES, S5_SLAB_ST), F32),
            pltpu.VMEM((S5_ROWS, WIDTH), F32),
        ],
        compiler_params=pltpu.CompilerParams(
            dimension_semantics=("arbitrary",), vmem_limit_bytes=VMEM_LIMIT),
        name="s5",
    )(p_tb, bre, bim, cre, cim, lre, lim, d, wglu, bglu)


CV_TL = 64
CV_ROWS = CV_TL * BATCH
CF_HALO = (CONF_WIDTH - 1) * BATCH
SC_HALO = (SCONV_WIDTH - 1) * BATCH
CV_RB = 32


def _conv_kernel(scb_ref, scc_ref, scv_ref, val_ref, gate_ref, scw_ref, cfw_ref, cfb_ref,
                 lng_ref, lnb_ref, ob_ref, oc_ref, wpad_ref, zpad_ref):
    @pl.when(pl.program_id(0) == 0)
    def _():
        wpad_ref[0:SC_HALO, :] = jnp.zeros((SC_HALO, WIDTH), F32)
        zpad_ref[0:CF_HALO, :] = jnp.zeros((CF_HALO, WIDTH), F32)

    wpad_ref[SC_HALO:SC_HALO + CV_ROWS, :] = scc_ref[...].astype(F32) * scv_ref[...].astype(F32)
    acc = scw_ref[0:1, :] * wpad_ref[0:CV_ROWS, :]
    for j in range(1, SCONV_WIDTH):
        acc = acc + scw_ref[j:j + 1, :] * wpad_ref[j * BATCH:j * BATCH + CV_ROWS, :]
    ob_ref[...] = (scb_ref[...].astype(F32) * acc).astype(BF16)
    wpad_ref[0:SC_HALO, :] = wpad_ref[CV_ROWS:CV_ROWS + SC_HALO, :]

    zpad_ref[CF_HALO:CF_HALO + CV_ROWS, :] = (
        val_ref[...].astype(F32) * _sigmoid(gate_ref[...].astype(F32)))

    def block(rb, carry):
        r0 = pl.multiple_of(rb * CV_RB, CV_RB)
        z = cfb_ref[...] + cfw_ref[0:1, :] * zpad_ref[pl.ds(r0, CV_RB), :]
        for j in range(1, CONF_WIDTH):
            z = z + cfw_ref[j:j + 1, :] * zpad_ref[pl.ds(r0 + j * BATCH, CV_RB), :]
        mu = jnp.mean(z, axis=-1, keepdims=True)
        zc = z - mu
        var = jnp.mean(zc * zc, axis=-1, keepdims=True)
        zn = zc * lax.rsqrt(var + EPS) * lng_ref[...] + lnb_ref[...]
        oc_ref[pl.ds(r0, CV_RB), :] = _silu(zn).astype(BF16)
        return carry

    lax.fori_loop(0, CV_ROWS // CV_RB, block, 0)
    zpad_ref[0:CF_HALO, :] = zpad_ref[CV_ROWS:CV_ROWS + CF_HALO, :]


def _convs(p_tb, cvw):
    scw, cfw, cfb, lng, lnb = cvw
    col = lambda c: pl.BlockSpec((CV_ROWS, WIDTH), lambda l, c=c: (l, c))
    out = jax.ShapeDtypeStruct((SEQ * BATCH, WIDTH), BF16)
    return pl.pallas_call(
        _conv_kernel,
        grid=(SEQ // CV_TL,),
        in_specs=[col(COL_SC_B), col(COL_SC_C), col(COL_SC_V), col(COL_CF_VAL), col(COL_CF_GATE),
                  _const_spec(scw.shape), _const_spec(cfw.shape), _const_spec(cfb.shape),
                  _const_spec(lng.shape), _const_spec(lnb.shape)],
        out_specs=[pl.BlockSpec((CV_ROWS, WIDTH), lambda l: (l, 0))] * 2,
        out_shape=[out, out],
        scratch_shapes=[
            pltpu.VMEM((SC_HALO + CV_ROWS, WIDTH), F32),
            pltpu.VMEM((CF_HALO + CV_ROWS, WIDTH), F32),
        ],
        compiler_params=pltpu.CompilerParams(
            dimension_semantics=("arbitrary",), vmem_limit_bytes=VMEM_LIMIT),
        name="convs",
    )(p_tb, p_tb, p_tb, p_tb, p_tb, scw, cfw, cfb, lng, lnb)


HG_TL = 256
HG_REF_ROW = CHUNK // 2


def _split3(x):
    hi = x.astype(BF16)
    r1 = x - hi.astype(F32)
    mid = r1.astype(BF16)
    lo = (r1 - mid.astype(F32)).astype(BF16)
    return hi, mid, lo


def _hgrn_kernel(q_ref, f_ref, i_ref, g_ref, loglb_ref, l1plb_ref, omlb_ref, ng_ref,
                 out_ref, st_ref):
    @pl.when(pl.program_id(1) == 0)
    def _():
        st_ref[...] = jnp.zeros_like(st_ref)

    row = lax.broadcasted_iota(jnp.int32, (CHUNK, CHUNK), 0)
    colm = lax.broadcasted_iota(jnp.int32, (CHUNK, CHUNK), 1)
    causal = colm <= row
    tri = jnp.where(causal, 1.0, 0.0).astype(BF16)

    for c in range(HG_TL // CHUNK):
        rows = slice(c * CHUNK, (c + 1) * CHUNK)
        x = f_ref[rows, :]
        e = jnp.exp(-jnp.abs(x))
        log_sig = jnp.minimum(x, 0.0) - jnp.log1p(e)
        sig_neg = jnp.where(x >= 0.0, e, 1.0) / (1.0 + e)
        kf = omlb_ref[...] * sig_neg
        a = loglb_ref[...]
        b = l1plb_ref[...] + log_sig
        log_f = jnp.maximum(a, b) + jnp.log1p(jnp.exp(-jnp.abs(a - b)))
        hi, mid, lo = _split3(log_f)
        g = _dot(tri, hi) + _dot(tri, mid) + _dot(tri, lo)
        g_ref_row = g[HG_REF_ROW:HG_REF_ROW + 1, :]
        g_last = g[CHUNK - 1:CHUNK, :]
        qf = _silu(q_ref[rows, :].astype(F32))
        q_in = (qf * jnp.exp(g - g_ref_row)).astype(BF16)
        k_in = (kf * jnp.exp(g_ref_row - g)).astype(BF16)
        q_st = (qf * jnp.exp(g)).astype(BF16)
        k_st = (kf * jnp.exp(g_last - g)).astype(BF16)
        decay = jnp.exp(g_last)
        v = i_ref[rows, :]
        gate = g_ref[rows, :].astype(F32)
        for h in range(HGRN_HEADS):
            lanes = slice(h * HGRN_DIM, (h + 1) * HGRN_DIM)
            scores = lax.dot_general(q_in[:, lanes], k_in[:, lanes],
                                     (((1,), (1,)), ((), ())), preferred_element_type=F32)
            scores = jnp.where(causal, scores, 0.0).astype(BF16)
            st = st_ref[h]
            o = _dot(scores, v[:, lanes]) + lax.dot_general(
                q_st[:, lanes], st.astype(BF16), (((1,), (1,)), ((), ())),
                preferred_element_type=F32)
            st_ref[h] = st * decay[:, lanes] + lax.dot_general(
                v[:, lanes], k_st[:, lanes], (((0,), (0,)), ((), ())),
                preferred_element_type=F32)
            o = o * lax.rsqrt(jnp.mean(o * o, axis=-1, keepdims=True) + EPS)
            o = o * ng_ref[:, lanes] * _silu(gate[:, lanes])
            out_ref[rows, lanes] = o.astype(BF16)


def _hgrn(p_bt, f_bt, hgw):
    loglb, l1plb, omlb, ng = hgw
    nb = IN_COLS // WIDTH
    col = lambda c: pl.BlockSpec((HG_TL, WIDTH), lambda b, l, c=c: (l, b * nb + c))
    return pl.pallas_call(
        _hgrn_kernel,
        grid=(BATCH, SEQ // HG_TL),
        in_specs=[col(COL_HG_Q),
                  pl.BlockSpec((HG_TL, WIDTH), lambda b, l: (l, b)),
                  col(COL_HG_I), col(COL_HG_G),
                  _const_spec(loglb.shape), _const_spec(l1plb.shape),
                  _const_spec(omlb.shape), _const_spec(ng.shape)],
        out_specs=pl.BlockSpec((HG_TL, WIDTH), lambda b, l: (l, b)),
        out_shape=jax.ShapeDtypeStruct((SEQ, BATCH * WIDTH), BF16),
        scratch_shapes=[pltpu.VMEM((HGRN_HEADS, HGRN_DIM, HGRN_DIM), F32)],
        compiler_params=pltpu.CompilerParams(
            dimension_semantics=("arbitrary", "arbitrary"), vmem_limit_bytes=VMEM_LIMIT),
        name="hgrn",
    )(p_bt, f_bt, p_bt, p_bt, loglb, l1plb, omlb, ng)


MG_TL = 512


def _merge_kernel(x_ref, a_ref, b_ref, c_ref, d_ref, g0_ref, g1_ref, g2_ref, g3_ref,
                  wb_ref, wo_ref, out_ref):
    merged = None
    for n, (br, gl) in enumerate(((a_ref, g0_ref), (b_ref, g1_ref), (c_ref, g2_ref), (d_ref, g3_ref))):
        term = _sigmoid(gl[...].astype(F32)) * _dot(br[...], wb_ref[n])
        merged = term if merged is None else merged + term
    out_ref[...] = x_ref[...] + _dot(merged.astype(BF16), wo_ref[...])


def _merge(x, br_a, br_b, br_c, br_d, p_bt, wb, wo):
    ng = IN_COLS // D_MODEL
    g0 = GATE_COL0 // D_MODEL
    br = pl.BlockSpec((MG_TL, WIDTH), lambda b, l: (l, b))
    gate = lambda n: pl.BlockSpec((MG_TL, D_MODEL), lambda b, l, n=n: (l, b * ng + g0 + n))
    xs = pl.BlockSpec((None, MG_TL, D_MODEL), lambda b, l: (b, l, 0))
    return pl.pallas_call(
        _merge_kernel,
        grid=(BATCH, SEQ // MG_TL),
        in_specs=[xs, br, br, br, br, gate(0), gate(1), gate(2), gate(3),
                  _const_spec(wb.shape), _const_spec(wo.shape)],
        out_specs=xs,
        out_shape=jax.ShapeDtypeStruct((BATCH, SEQ, D_MODEL), F32),
        compiler_params=pltpu.CompilerParams(
            dimension_semantics=("arbitrary", "arbitrary"), vmem_limit_bytes=VMEM_LIMIT),
        name="merge",
    )(x, br_a, br_b, br_c, br_d, p_bt, p_bt, p_bt, p_bt, wb, wo)


FF_TL = 512
FF_CHUNKS = ((0, 1024), (1024, 1024), (2048, 768))


def _ffn_kernel(x_ref, g_ref, wg_ref, wu_ref, wd_ref, fg_ref, out_ref, *, final):
    x = x_ref[...]
    h = _rms_norm(x, g_ref[...]).astype(BF16)
    acc = x
    for c0, cn in FF_CHUNKS:
        gate = _dot(h, wg_ref[:, c0:c0 + cn])
        up = _dot(h, wu_ref[:, c0:c0 + cn])
        acc = acc + _dot((_silu(gate) * up).astype(BF16), wd_ref[c0:c0 + cn, :])
    if final:
        acc = _rms_norm(acc, fg_ref[...])
    out_ref[...] = acc


def _ffn(x, g, wg, wu, wd, fg, final):
    xs = pl.BlockSpec((None, FF_TL, D_MODEL), lambda b, l: (b, l, 0))
    return pl.pallas_call(
        functools.partial(_ffn_kernel, final=final),
        grid=(BATCH, SEQ // FF_TL),
        in_specs=[xs, _const_spec(g.shape), _const_spec(wg.shape), _const_spec(wu.shape),
                  _const_spec(wd.shape), _const_spec(fg.shape)],
        out_specs=xs,
        out_shape=jax.ShapeDtypeStruct((BATCH, SEQ, D_MODEL), F32),
        compiler_params=pltpu.CompilerParams(
            dimension_semantics=("arbitrary", "arbitrary"), vmem_limit_bytes=VMEM_LIMIT),
        name="ffn",
    )(x, g, wg, wu, wd, fg)


def _s5_params(lam_re, lam_im, log_dt, b_re, b_im, c_re, c_im, d, w_glu, b_glu):
    lam = lax.complex(lam_re, lam_im)
    dt = jnp.exp(log_dt)[:, None]
    lam_bar = jnp.exp(lam * dt)
    b_bar = ((lam_bar - 1.0) / lam)[..., None] * lax.complex(b_re, b_im)
    gps = S5_GROUPS // S5_SLABS
    eye = jnp.eye(gps, dtype=F32)

    def in_mat(b):
        b = b.reshape(S5_SLABS, gps, S5_STATE, S5_GROUP)
        m = jnp.einsum('sgpc,gh->sgchp', b, eye)
        return m.reshape(S5_SLABS, S5_SLAB_CH, S5_SLAB_ST).astype(BF16)

    def out_mat(c):
        c = c.reshape(S5_SLABS, gps, S5_GROUP, S5_STATE)
        m = jnp.einsum('sgcp,gh->sgphc', c, eye)
        return m.reshape(S5_SLABS, S5_SLAB_ST, S5_SLAB_CH).astype(BF16)

    lre = jnp.real(lam_bar).reshape(S5_SLABS, 1, S5_SLAB_ST)
    lim = jnp.imag(lam_bar).reshape(S5_SLABS, 1, S5_SLAB_ST)
    return (in_mat(jnp.real(b_bar)), in_mat(jnp.imag(b_bar)), out_mat(c_re), out_mat(-c_im),
            lre, lim, d.reshape(1, WIDTH), w_glu.astype(BF16), b_glu.reshape(1, WIDTH))


def _hgrn_lower_bounds(logits):
    p = jax.nn.softmax(logits.astype(F32), axis=0)
    cs = jnp.cumsum(p, axis=0)
    return cs - cs[0:1]


def kernel(x, mix_norm_g, w_in, s5_lambda_re, s5_lambda_im, s5_log_dt, s5_b_re, s5_b_im,
           s5_c_re, s5_c_im, s5_d, s5_w_glu, s5_b_glu, sconv_w, conf_dw_w, conf_dw_b,
           conf_ln_g, conf_ln_b, hgrn_lb_logits, hgrn_norm_g, w_branch, w_out,
           ffn_norm_g, ffn_w_gate, ffn_w_up, ffn_w_down, final_norm_g):
    lower_bounds = _hgrn_lower_bounds(hgrn_lb_logits)
    row = lambda v: v.reshape(1, -1)
    for l in range(DEPTH):
        p_bt, f_bt = _proj(x, row(mix_norm_g[l]), w_in[l].astype(BF16))
        p_tb = p_bt.reshape(SEQ * BATCH, IN_COLS)
        s5w = _s5_params(s5_lambda_re[l], s5_lambda_im[l], s5_log_dt[l], s5_b_re[l], s5_b_im[l],
                         s5_c_re[l], s5_c_im[l], s5_d[l], s5_w_glu[l], s5_b_glu[l])
        br_a = _s5(p_tb, s5w)
        br_b, br_c = _convs(p_tb, (sconv_w[l], conf_dw_w[l], row(conf_dw_b[l]),
                                   row(conf_ln_g[l]), row(conf_ln_b[l])))
        lb = lower_bounds[l]
        br_d = _hgrn(p_bt, f_bt, (row(jnp.log(lb)), row(jnp.log1p(-lb)), row(1.0 - lb),
                                  row(hgrn_norm_g[l])))
        tb = lambda a: a.reshape(SEQ, BATCH * WIDTH)
        x = _merge(x, tb(br_a), tb(br_b), tb(br_c), br_d, p_bt,
                   w_branch[l].astype(BF16), w_out[l].astype(BF16))
        x = _ffn(x, row(ffn_norm_g[l]), ffn_w_gate[l].astype(BF16), ffn_w_up[l].astype(BF16),
                 ffn_w_down[l].astype(BF16), row(final_norm_g), final=(l == DEPTH - 1))
    return x
```

```python
import functools
import math

import jax
import jax.numpy as jnp
from jax import lax
from jax.experimental import pallas as pl
from jax.experimental.pallas import tpu as pltpu

F32 = jnp.float32
BF16 = jnp.bfloat16

D_MODEL = 1024
BATCH = 8
SEQ = 4096
ROWS = SEQ * BATCH
DEPTH = 2
CHUNK = 64
N_BRANCHES = 4
WIDTH = D_MODEL // 2
S5_GROUP = 16
S5_GROUPS = WIDTH // S5_GROUP
S5_STATE = 64
S5_SLABS = 4
S5_SLAB_CH = WIDTH // S5_SLABS
S5_SLAB_ST = S5_SLAB_CH // S5_GROUP * S5_STATE
SCONV_WIDTH = 3
CONF_WIDTH = 31
HGRN_HEADS = 4
HGRN_DIM = WIDTH // HGRN_HEADS
D_FF = 2816
IN_COLS = 10 * WIDTH + N_BRANCHES * D_MODEL
EPS = 1e-6

SUBLANES = 8
LANES = 128
VMEM_LIMIT = 56 * 1024 * 1024

COL_S5, COL_SC_B, COL_SC_C, COL_SC_V, COL_CF_VAL, COL_CF_GATE = 0, 1, 2, 3, 4, 5
COL_HG_Q, COL_HG_F, COL_HG_I, COL_HG_G = 6, 7, 8, 9
GATE_COL0 = 10 * WIDTH


def _dot(a, b):
    return jnp.dot(a, b, preferred_element_type=F32)


def _rms_norm(x, g):
    return x * lax.rsqrt(jnp.mean(x * x, axis=-1, keepdims=True) + EPS) * g


def _sigmoid(x):
    return 0.5 + 0.5 * jnp.tanh(0.5 * x)


def _silu(x):
    return x * _sigmoid(x)


def _const_spec(shape):
    n = len(shape)
    return pl.BlockSpec(shape, lambda *_: (0,) * n, pipeline_mode=pl.Buffered(1))


def _params(*semantics):
    return pltpu.CompilerParams(dimension_semantics=semantics, vmem_limit_bytes=VMEM_LIMIT)


PROJ_TL = 1024
PROJ_TN = 2304
F_TILE = (COL_HG_F * WIDTH) // PROJ_TN
F_OFF = COL_HG_F * WIDTH - F_TILE * PROJ_TN


def _proj_kernel(x_ref, g_ref, w_ref, p_ref, f_ref, h_ref):
    n = pl.program_id(1)

    @pl.when(n == 0)
    def _():
        h_ref[...] = _rms_norm(x_ref[...], g_ref[...]).astype(BF16)

    acc = _dot(h_ref[...], w_ref[...])
    p_ref[...] = acc.astype(BF16)

    @pl.when(n == F_TILE)
    def _():
        f_ref[...] = acc[:, F_OFF:F_OFF + WIDTH]


def _proj(x, g, w):
    return pl.pallas_call(
        _proj_kernel,
        grid=(ROWS // PROJ_TL, IN_COLS // PROJ_TN),
        in_specs=[
            pl.BlockSpec((PROJ_TL, D_MODEL), lambda r, n: (r, 0)),
            pl.BlockSpec((1, D_MODEL), lambda r, n: (0, 0)),
            pl.BlockSpec((D_MODEL, PROJ_TN), lambda r, n: (0, n)),
        ],
        out_specs=[
            pl.BlockSpec((PROJ_TL, PROJ_TN), lambda r, n: (r, n)),
            pl.BlockSpec((PROJ_TL, WIDTH), lambda r, n: (r, 0)),
        ],
        out_shape=[
            jax.ShapeDtypeStruct((ROWS, IN_COLS), BF16),
            jax.ShapeDtypeStruct((ROWS, WIDTH), F32),
        ],
        scratch_shapes=[pltpu.VMEM((PROJ_TL, D_MODEL), BF16)],
        compiler_params=_params("arbitrary", "arbitrary"),
        name="proj",
    )(x, g, w)


S5_TL = 64
S5_ROWS = S5_TL * BATCH


def _gelu_tanh(y):
    c = math.sqrt(2.0 / math.pi)
    return 0.5 * y * (1.0 + jnp.tanh(c * (y + 0.044715 * (y * y * y))))


def _s5_kernel(u_ref, bre_ref, bim_ref, cre_ref, cim_ref, lre_ref, lim_ref, d_ref,
               wglu_ref, bglu_ref, out_ref, xr_ref, xi_ref, st_ref, y_ref):
    @pl.when(pl.program_id(0) == 0)
    def _():
        st_ref[...] = jnp.zeros_like(st_ref)

    for i in range(S5_SLABS):
        lanes = slice(S5_SLAB_CH * i, S5_SLAB_CH * (i + 1))
        ui = u_ref[:, lanes]
        xr_ref[...] = _dot(ui, bre_ref[i])
        xi_ref[...] = _dot(ui, bim_ref[i])
        lr = jnp.broadcast_to(lre_ref[i], (SUBLANES, S5_SLAB_ST))
        li = jnp.broadcast_to(lim_ref[i], (SUBLANES, S5_SLAB_ST))

        def step(t, carry, lr=lr, li=li):
            sr, si = carry
            r0 = pl.multiple_of(t * SUBLANES, SUBLANES)
            nr = lr * sr - li * si + xr_ref[pl.ds(r0, SUBLANES), :]
            ni = lr * si + li * sr + xi_ref[pl.ds(r0, SUBLANES), :]
            xr_ref[pl.ds(r0, SUBLANES), :] = nr
            xi_ref[pl.ds(r0, SUBLANES), :] = ni
            return nr, ni

        sr, si = lax.fori_loop(0, S5_TL, step, (st_ref[i, 0], st_ref[i, 1]), unroll=4)
        st_ref[i, 0] = sr
        st_ref[i, 1] = si
        y_ref[:, lanes] = (_dot(xr_ref[...].astype(BF16), cre_ref[i])
                           + _dot(xi_ref[...].astype(BF16), cim_ref[i]))

    y = y_ref[...] + d_ref[...] * u_ref[...].astype(F32)
    y = _gelu_tanh(y)
    z = _dot(y.astype(BF16), wglu_ref[...]) + bglu_ref[...]
    out_ref[...] = (y * _sigmoid(z)).astype(BF16)


def _s5(p, s5w):
    bre, bim, cre, cim, lre, lim, d, wglu, bglu = s5w
    return pl.pallas_call(
        _s5_kernel,
        grid=(SEQ // S5_TL,),
        in_specs=[
            pl.BlockSpec((S5_ROWS, WIDTH), lambda l: (l, COL_S5)),
            _const_spec(bre.shape), _const_spec(bim.shape),
            _const_spec(cre.shape), _const_spec(cim.shape),
            _const_spec(lre.shape), _const_spec(lim.shape),
            _const_spec(d.shape), _const_spec(wglu.shape), _const_spec(bglu.shape),
        ],
        out_specs=pl.BlockSpec((S5_ROWS, WIDTH), lambda l: (l, 0)),
        out_shape=jax.ShapeDtypeStruct((ROWS, WIDTH), BF16),
        scratch_shapes=[
            pltpu.VMEM((S5_ROWS, S5_SLAB_ST), F32),
            pltpu.VMEM((S5_ROWS, S5_SLAB_ST), F32),
            pltpu.VMEM((S5_SLABS, 2, SUBLANES, S5_SLAB_ST), F32),
            pltpu.VMEM((S5_ROWS, WIDTH), F32),
        ],
        compiler_params=_params("arbitrary"),
        name="s5",
    )(p, bre, bim, cre, cim, lre, lim, d, wglu, bglu)


CV_TL = 64
CV_ROWS = CV_TL * BATCH
CF_HALO = (CONF_WIDTH - 1) * BATCH
SC_HALO = (SCONV_WIDTH - 1) * BATCH
CV_TB = 4


def _conv_kernel(scb_ref, scc_ref, scv_ref, val_ref, gate_ref, scw_ref, cfw_ref, cfb_ref,
                 lng_ref, lnb_ref, ob_ref, oc_ref, wpad_ref, zpad_ref, conv_ref):
    @pl.when(pl.program_id(0) == 0)
    def _():
        wpad_ref[0:SC_HALO, :] = jnp.zeros((SC_HALO, WIDTH), F32)
        zpad_ref[0:CF_HALO, :] = jnp.zeros((CF_HALO, WIDTH), F32)

    wpad_ref[SC_HALO:SC_HALO + CV_ROWS, :] = scc_ref[...].astype(F32) * scv_ref[...].astype(F32)
    acc = scw_ref[0:1, :] * wpad_ref[0:CV_ROWS, :]
    for j in range(1, SCONV_WIDTH):
        acc = acc + scw_ref[j:j + 1, :] * wpad_ref[j * BATCH:j * BATCH + CV_ROWS, :]
    ob_ref[...] = (scb_ref[...].astype(F32) * acc).astype(BF16)
    wpad_ref[0:SC_HALO, :] = wpad_ref[CV_ROWS:CV_ROWS + SC_HALO, :]

    zpad_ref[CF_HALO:CF_HALO + CV_ROWS, :] = (
        val_ref[...].astype(F32) * _sigmoid(gate_ref[...].astype(F32)))

    for s in range(WIDTH // LANES):
        lanes = slice(s * LANES, (s + 1) * LANES)
        taps = [jnp.broadcast_to(cfw_ref[j:j + 1, lanes], (SUBLANES, LANES))
                for j in range(CONF_WIDTH)]
        bias = jnp.broadcast_to(cfb_ref[:, lanes], (SUBLANES, LANES))

        def block(tb, carry, lanes=lanes, taps=taps, bias=bias):
            r0 = pl.multiple_of(tb * (CV_TB * BATCH), CV_TB * BATCH)
            accs = [bias] * CV_TB
            for s_in in range(CV_TB + CONF_WIDTH - 1):
                zv = zpad_ref[pl.ds(r0 + s_in * BATCH, BATCH), lanes]
                for o in range(CV_TB):
                    j = s_in - o
                    if 0 <= j < CONF_WIDTH:
                        accs[o] = accs[o] + taps[j] * zv
            for o in range(CV_TB):
                conv_ref[pl.ds(r0 + o * BATCH, BATCH), lanes] = accs[o]
            return carry

        lax.fori_loop(0, CV_TL // CV_TB, block, 0)

    z = conv_ref[...]
    mu = jnp.mean(z, axis=-1, keepdims=True)
    zc = z - mu
    var = jnp.mean(zc * zc, axis=-1, keepdims=True)
    zn = zc * lax.rsqrt(var + EPS) * lng_ref[...] + lnb_ref[...]
    oc_ref[...] = _silu(zn).astype(BF16)
    zpad_ref[0:CF_HALO, :] = zpad_ref[CV_ROWS:CV_ROWS + CF_HALO, :]


def _convs(p, cvw):
    scw, cfw, cfb, lng, lnb = cvw
    col = lambda c: pl.BlockSpec((CV_ROWS, WIDTH), lambda l, c=c: (l, c))
    out = jax.ShapeDtypeStruct((ROWS, WIDTH), BF16)
    return pl.pallas_call(
        _conv_kernel,
        grid=(SEQ // CV_TL,),
        in_specs=[col(COL_SC_B), col(COL_SC_C), col(COL_SC_V), col(COL_CF_VAL), col(COL_CF_GATE),
                  _const_spec(scw.shape), _const_spec(cfw.shape), _const_spec(cfb.shape),
                  _const_spec(lng.shape), _const_spec(lnb.shape)],
        out_specs=[pl.BlockSpec((CV_ROWS, WIDTH), lambda l: (l, 0))] * 2,
        out_shape=[out, out],
        scratch_shapes=[
            pltpu.VMEM((SC_HALO + CV_ROWS, WIDTH), F32),
            pltpu.VMEM((CF_HALO + CV_ROWS, WIDTH), F32),
            pltpu.VMEM((CV_ROWS, WIDTH), F32),
        ],
        compiler_params=_params("arbitrary"),
        name="convs",
    )(p, p, p, p, p, scw, cfw, cfb, lng, lnb)


HG_ROWS = CHUNK * BATCH
HG_REF_STEP = CHUNK // 2


def _hgrn_kernel(q_ref, f_ref, i_ref, g_ref, loglb_ref, l1plb_ref, omlb_ref, ng_ref,
                 out_ref, st_ref, g_s, qin_s, kin_s, qst_s, kst_s, v_s, dec_s, o_s):
    @pl.when(pl.program_id(0) == 0)
    def _():
        st_ref[...] = jnp.zeros_like(st_ref)

    x = f_ref[...]
    e = jnp.exp(-jnp.abs(x))
    ope = 1.0 + e
    log_sig = jnp.minimum(x, 0.0) - jnp.log(ope)
    sig_neg = jnp.where(x >= 0.0, e, 1.0) / ope
    kf = omlb_ref[...] * sig_neg
    a = loglb_ref[...]
    b = l1plb_ref[...] + log_sig
    log_f = jnp.maximum(a, b) + jnp.log(1.0 + jnp.exp(-jnp.abs(a - b)))

    run = jnp.zeros((BATCH, WIDTH), F32)
    for t in range(CHUNK):
        run = run + log_f[t * BATCH:(t + 1) * BATCH, :]
        g_s[t * BATCH:(t + 1) * BATCH, :] = run
    g = g_s[...]
    g_mid = pltpu.repeat(g_s[HG_REF_STEP * BATCH:(HG_REF_STEP + 1) * BATCH, :], CHUNK, axis=0)
    g_last8 = g_s[(CHUNK - 1) * BATCH:CHUNK * BATCH, :]
    g_last = pltpu.repeat(g_last8, CHUNK, axis=0)
    qf = _silu(q_ref[...].astype(F32))
    def put(dst, val):
        for h in range(HGRN_HEADS):
            dst[h] = val[:, h * HGRN_DIM:(h + 1) * HGRN_DIM]

    put(qin_s, qf * jnp.exp(g - g_mid))
    put(kin_s, kf * jnp.exp(g_mid - g))
    put(qst_s, qf * jnp.exp(g))
    put(kst_s, kf * jnp.exp(g_last - g))
    put(v_s, i_ref[...].astype(F32))
    dec_s[...] = jnp.exp(g_last8)

    row = lax.broadcasted_iota(jnp.int32, (CHUNK, CHUNK), 0)
    colm = lax.broadcasted_iota(jnp.int32, (CHUNK, CHUNK), 1)
    causal = colm <= row

    def per_batch(bi, carry):
        rows = pl.ds(bi, CHUNK, stride=BATCH)
        decay = dec_s[pl.ds(bi, 1), :]
        for h in range(HGRN_HEADS):
            lanes = slice(h * HGRN_DIM, (h + 1) * HGRN_DIM)
            q_in = qin_s[h, rows, :].astype(BF16)
            k_in = kin_s[h, rows, :].astype(BF16)
            q_st = qst_s[h, rows, :].astype(BF16)
            k_st = kst_s[h, rows, :].astype(BF16)
            v = v_s[h, rows, :].astype(BF16)
            scores = lax.dot_general(q_in, k_in, (((1,), (1,)), ((), ())),
                                     preferred_element_type=F32)
            scores = jnp.where(causal, scores, 0.0).astype(BF16)
            st = st_ref[bi, h]
            o_s[h, rows, :] = _dot(scores, v) + lax.dot_general(
                q_st, st.astype(BF16), (((1,), (1,)), ((), ())), preferred_element_type=F32)
            st_ref[bi, h] = st * decay[:, lanes] + lax.dot_general(
                v, k_st, (((0,), (0,)), ((), ())), preferred_element_type=F32)
        return carry

    lax.fori_loop(0, BATCH, per_batch, 0)

    for h in range(HGRN_HEADS):
        lanes = slice(h * HGRN_DIM, (h + 1) * HGRN_DIM)
        o = o_s[h]
        o = o * lax.rsqrt(jnp.mean(o * o, axis=-1, keepdims=True) + EPS)
        out_ref[:, lanes] = (o * ng_ref[:, lanes]
                             * _silu(g_ref[:, lanes].astype(F32))).astype(BF16)


def _hgrn(p, f, hgw):
    loglb, l1plb, omlb, ng = hgw
    col = lambda c: pl.BlockSpec((HG_ROWS, WIDTH), lambda l, c=c: (l, c))
    big = pltpu.VMEM((HG_ROWS, WIDTH), F32)
    split = pltpu.VMEM((HGRN_HEADS, HG_ROWS, HGRN_DIM), F32)
    return pl.pallas_call(
        _hgrn_kernel,
        grid=(SEQ // CHUNK,),
        in_specs=[col(COL_HG_Q), pl.BlockSpec((HG_ROWS, WIDTH), lambda l: (l, 0)),
                  col(COL_HG_I), col(COL_HG_G),
                  _const_spec(loglb.shape), _const_spec(l1plb.shape),
                  _const_spec(omlb.shape), _const_spec(ng.shape)],
        out_specs=pl.BlockSpec((HG_ROWS, WIDTH), lambda l: (l, 0)),
        out_shape=jax.ShapeDtypeStruct((ROWS, WIDTH), BF16),
        scratch_shapes=[pltpu.VMEM((BATCH, HGRN_HEADS, HGRN_DIM, HGRN_DIM), F32),
                        big, split, split, split, split, split,
                        pltpu.VMEM((BATCH, WIDTH), F32), split],
        compiler_params=_params("arbitrary"),
        name="hgrn",
    )(p, f, p, p, loglb, l1plb, omlb, ng)


MG_TL = 512


def _merge_kernel(x_ref, a_ref, b_ref, c_ref, d_ref, g0_ref, g1_ref, g2_ref, g3_ref,
                  wb_ref, wo_ref, out_ref):
    merged = None
    for n, (br, gl) in enumerate(((a_ref, g0_ref), (b_ref, g1_ref), (c_ref, g2_ref), (d_ref, g3_ref))):
        term = _sigmoid(gl[...].astype(F32)) * _dot(br[...], wb_ref[n])
        merged = term if merged is None else merged + term
    out_ref[...] = x_ref[...] + _dot(merged.astype(BF16), wo_ref[...])


def _merge(x, br_a, br_b, br_c, br_d, p, wb, wo):
    g0 = GATE_COL0 // D_MODEL
    br = pl.BlockSpec((MG_TL, WIDTH), lambda r: (r, 0))
    gate = lambda n: pl.BlockSpec((MG_TL, D_MODEL), lambda r, n=n: (r, g0 + n))
    xs = pl.BlockSpec((MG_TL, D_MODEL), lambda r: (r, 0))
    return pl.pallas_call(
        _merge_kernel,
        grid=(ROWS // MG_TL,),
        in_specs=[xs, br, br, br, br, gate(0), gate(1), gate(2), gate(3),
                  _const_spec(wb.shape), _const_spec(wo.shape)],
        out_specs=xs,
        out_shape=jax.ShapeDtypeStruct((ROWS, D_MODEL), F32),
        compiler_params=_params("arbitrary"),
        name="merge",
    )(x, br_a, br_b, br_c, br_d, p, p, p, p, wb, wo)


FF_TL = 512
FF_CHUNKS = ((0, 1024), (1024, 1024), (2048, 768))


def _ffn_kernel(x_ref, g_ref, wg_ref, wu_ref, wd_ref, fg_ref, out_ref, *, final):
    x = x_ref[...]
    h = _rms_norm(x, g_ref[...]).astype(BF16)
    acc = x
    for c0, cn in FF_CHUNKS:
        gate = _dot(h, wg_ref[:, c0:c0 + cn])
        up = _dot(h, wu_ref[:, c0:c0 + cn])
        acc = acc + _dot((_silu(gate) * up).astype(BF16), wd_ref[c0:c0 + cn, :])
    if final:
        acc = _rms_norm(acc, fg_ref[...])
    out_ref[...] = acc


def _ffn(x, g, wg, wu, wd, fg, final):
    xs = pl.BlockSpec((FF_TL, D_MODEL), lambda r: (r, 0))
    return pl.pallas_call(
        functools.partial(_ffn_kernel, final=final),
        grid=(ROWS // FF_TL,),
        in_specs=[xs, _const_spec(g.shape), _const_spec(wg.shape), _const_spec(wu.shape),
                  _const_spec(wd.shape), _const_spec(fg.shape)],
        out_specs=xs,
        out_shape=jax.ShapeDtypeStruct((ROWS, D_MODEL), F32),
        compiler_params=_params("arbitrary"),
        name="ffn",
    )(x, g, wg, wu, wd, fg)


def _s5_params(lam_re, lam_im, log_dt, b_re, b_im, c_re, c_im, d, w_glu, b_glu):
    dt = jnp.exp(log_dt)[:, None]
    mag = jnp.exp(lam_re * dt)
    lbr = mag * jnp.cos(lam_im * dt)
    lbi = mag * jnp.sin(lam_im * dt)
    den = lam_re * lam_re + lam_im * lam_im
    cr = ((lbr - 1.0) * lam_re + lbi * lam_im) / den
    ci = (lbi * lam_re - (lbr - 1.0) * lam_im) / den
    bbr = cr[..., None] * b_re - ci[..., None] * b_im
    bbi = cr[..., None] * b_im + ci[..., None] * b_re
    gps = S5_GROUPS // S5_SLABS
    eye = jnp.eye(gps, dtype=F32)

    def in_mat(b):
        b = b.reshape(S5_SLABS, gps, S5_STATE, S5_GROUP)
        m = jnp.einsum('sgpc,gh->sgchp', b, eye)
        return m.reshape(S5_SLABS, S5_SLAB_CH, S5_SLAB_ST).astype(BF16)

    def out_mat(c):
        c = c.reshape(S5_SLABS, gps, S5_GROUP, S5_STATE)
        m = jnp.einsum('sgcp,gh->sgphc', c, eye)
        return m.reshape(S5_SLABS, S5_SLAB_ST, S5_SLAB_CH).astype(BF16)

    lre = lbr.reshape(S5_SLABS, 1, S5_SLAB_ST)
    lim = lbi.reshape(S5_SLABS, 1, S5_SLAB_ST)
    return (in_mat(bbr), in_mat(bbi), out_mat(c_re), out_mat(-c_im),
            lre, lim, d.reshape(1, WIDTH), w_glu.astype(BF16), b_glu.reshape(1, WIDTH))


def _hgrn_lower_bounds(logits):
    p = jax.nn.softmax(logits.astype(F32), axis=0)
    cs = jnp.cumsum(p, axis=0)
    return cs - cs[0:1]


def kernel(x, mix_norm_g, w_in, s5_lambda_re, s5_lambda_im, s5_log_dt, s5_b_re, s5_b_im,
           s5_c_re, s5_c_im, s5_d, s5_w_glu, s5_b_glu, sconv_w, conf_dw_w, conf_dw_b,
           conf_ln_g, conf_ln_b, hgrn_lb_logits, hgrn_norm_g, w_branch, w_out,
           ffn_norm_g, ffn_w_gate, ffn_w_up, ffn_w_down, final_norm_g):
    lower_bounds = _hgrn_lower_bounds(hgrn_lb_logits)
    row = lambda v: v.reshape(1, -1)
    x = x.transpose(1, 0, 2).reshape(ROWS, D_MODEL)
    for l in range(DEPTH):
        p, f = _proj(x, row(mix_norm_g[l]), w_in[l].astype(BF16))
        s5w = _s5_params(s5_lambda_re[l], s5_lambda_im[l], s5_log_dt[l], s5_b_re[l], s5_b_im[l],
                         s5_c_re[l], s5_c_im[l], s5_d[l], s5_w_glu[l], s5_b_glu[l])
        br_a = _s5(p, s5w)
        br_b, br_c = _convs(p, (sconv_w[l], conf_dw_w[l], row(conf_dw_b[l]),
                                row(conf_ln_g[l]), row(conf_ln_b[l])))
        lb = lower_bounds[l]
        br_d = _hgrn(p, f, (row(jnp.log(lb)), row(jnp.log1p(-lb)), row(1.0 - lb),
                            row(hgrn_norm_g[l])))
        x = _merge(x, br_a, br_b, br_c, br_d, p,
                   w_branch[l].astype(BF16), w_out[l].astype(BF16))
        x = _ffn(x, row(ffn_norm_g[l]), ffn_w_gate[l].astype(BF16), ffn_w_up[l].astype(BF16),
                 ffn_w_down[l].astype(BF16), row(final_norm_g), final=(l == DEPTH - 1))
    return x.reshape(SEQ, BATCH, D_MODEL).transpose(1, 0, 2)
```
